```python
import jax, jax.numpy as jnp
from jax import lax
import numpy as np

D_MODEL = 1024
BATCH = 1
SEQ = 16384
DEPTH = 4

CHUNK = 64
N_MIXERS = 3
N_LAYERS_A = (DEPTH + 2) // 3
N_LAYERS_B = (DEPTH + 1) // 3
N_LAYERS_C = DEPTH // 3
SHORT_CONV_WIDTH = 3
POOL_WINDOWS = (2, 4, 8, 16)
N_POOL_GROUPS = len(POOL_WINDOWS)
POOL_GROUP_DIM = D_MODEL // N_POOL_GROUPS
CONFORMER_CONV_WIDTH = 31
FFN_DIM = 2816
FFN_CONV_WIDTH = 3
RMS_EPS = 1e-6
LN_EPS = 1e-5

kernel_name = "hybrid_conv_pool_conformer_trunk"


def rmsnorm(x, g):
    xf = x.astype(jnp.float32)
    y = xf * lax.rsqrt(jnp.mean(xf * xf, axis=-1, keepdims=True) + RMS_EPS)
    return (y * g.astype(jnp.float32)).astype(x.dtype)


def layernorm(x, g, b):
    xf = x.astype(jnp.float32)
    mu = jnp.mean(xf, axis=-1, keepdims=True)
    var = jnp.mean(jnp.square(xf - mu), axis=-1, keepdims=True)
    y = (xf - mu) * lax.rsqrt(var + LN_EPS)
    return (y * g.astype(jnp.float32) + b.astype(jnp.float32)).astype(x.dtype)


def causal_depthwise_conv(x, w, b=None):
    k, ch = w.shape
    xp = jnp.pad(x, ((0, 0), (k - 1, 0), (0, 0)))
    y = lax.conv_general_dilated(
        xp, w[:, None, :].astype(x.dtype), window_strides=(1,), padding="VALID",
        dimension_numbers=("NWC", "WIO", "NWC"), feature_group_count=ch)
    if b is not None:
        y = y + b
    return y


def short_conv_mixer(h, w_in, w_conv, w_out):
    bgate, cgate, hin = jnp.split(h @ w_in, 3, axis=-1)
    u = causal_depthwise_conv(cgate * hin, w_conv)
    return (bgate * u) @ w_out


def trailing_mean(xf, window):
    s = xf.shape[1]
    csum = jnp.cumsum(xf, axis=1)
    shifted = jnp.pad(csum, ((0, 0), (window, 0), (0, 0)))[:, :s]
    count = jnp.minimum(jnp.arange(1, s + 1), window).astype(jnp.float32)[None, :, None]
    return (csum - shifted) / count


def pooling_mixer(h, w_group, b_group, scale):
    bsz, s, d = h.shape
    hf = h.astype(jnp.float32).reshape(bsz, s, N_POOL_GROUPS, POOL_GROUP_DIM)
    pooled = jnp.stack(
        [trailing_mean(hf[:, :, g], w) - hf[:, :, g] for g, w in enumerate(POOL_WINDOWS)],
        axis=2).astype(h.dtype)
    y = jnp.einsum("bsgc,gcd->bsgd", pooled, w_group).reshape(bsz, s, d)
    return (y + b_group) * scale


def conformer_conv_module(h, w_pw1, b_pw1, w_dw, b_dw, ln_g, ln_b, w_pw2, b_pw2):
    a, g = jnp.split(h @ w_pw1 + b_pw1, 2, axis=-1)
    u = a * jax.nn.sigmoid(g)
    u = causal_depthwise_conv(u, w_dw, b_dw)
    u = jax.nn.silu(layernorm(u, ln_g, ln_b))
    return u @ w_pw2 + b_pw2


def conv_ffn(h, w_up, w_conv, b_conv, w_down):
    gate, val = jnp.split(h @ w_up, 2, axis=-1)
    gate = causal_depthwise_conv(gate, w_conv, b_conv)
    return (jax.nn.silu(gate) * val) @ w_down


def setup_inputs(seed: int = 0) -> dict:
    key = jax.random.key(seed)
    ks = iter(jax.random.split(key, 32))
    D = D_MODEL

    def nrm(shape, scale):
        return jax.random.normal(next(ks), shape, jnp.float32) * scale

    return {
        "x": nrm((BATCH, SEQ, D), 1.0),
        "c": nrm((BATCH, D), 1.0),
        "w_mod": nrm((DEPTH, D, 6 * D), 0.5 * D ** -0.5),
        "b_mod": nrm((DEPTH, 6 * D), 0.02),
        "norm_g": 1.0 + nrm((DEPTH, 4, D), 0.05),
        "sc_w_in": nrm((N_LAYERS_A, D, 3 * D), D ** -0.5),
        "sc_conv": nrm((N_LAYERS_A, SHORT_CONV_WIDTH, D), SHORT_CONV_WIDTH ** -0.5),
        "sc_w_out": nrm((N_LAYERS_A, D, D), D ** -0.5),
        "pool_w": nrm((N_LAYERS_B, N_POOL_GROUPS, POOL_GROUP_DIM, POOL_GROUP_DIM), POOL_GROUP_DIM ** -0.5),
        "pool_b": nrm((N_LAYERS_B, D), 0.02),
        "pool_scale": 1.0 + nrm((N_LAYERS_B, D), 0.05),
        "cf_w_pw1": nrm((N_LAYERS_C, D, 2 * D), D ** -0.5),
        "cf_b_pw1": nrm((N_LAYERS_C, 2 * D), 0.02),
        "cf_w_dw": nrm((N_LAYERS_C, CONFORMER_CONV_WIDTH, D), CONFORMER_CONV_WIDTH ** -0.5),
        "cf_b_dw": nrm((N_LAYERS_C, D), 0.02),
        "cf_ln_g": 1.0 + nrm((N_LAYERS_C, D), 0.05),
        "cf_ln_b": nrm((N_LAYERS_C, D), 0.02),
        "cf_w_pw2": nrm((N_LAYERS_C, D, D), D ** -0.5),
        "cf_b_pw2": nrm((N_LAYERS_C, D), 0.02),
        "ffn_w_up": nrm((DEPTH, D, 2 * FFN_DIM), D ** -0.5),
        "ffn_conv": nrm((DEPTH, FFN_CONV_WIDTH, FFN_DIM), FFN_CONV_WIDTH ** -0.5),
        "ffn_b_conv": nrm((DEPTH, FFN_DIM), 0.02),
        "ffn_w_down": nrm((DEPTH, FFN_DIM, D), FFN_DIM ** -0.5),
    }


def reference(x, c, w_mod, b_mod, norm_g, sc_w_in, sc_conv, sc_w_out, pool_w, pool_b, pool_scale,
              cf_w_pw1, cf_b_pw1, cf_w_dw, cf_b_dw, cf_ln_g, cf_ln_b, cf_w_pw2, cf_b_pw2,
              ffn_w_up, ffn_conv, ffn_b_conv, ffn_w_down):
    c_act = jax.nn.silu(c)
    for i in range(DEPTH):
        mod = (c_act @ w_mod[i] + b_mod[i])[:, None, :]
        sh1, sc1, g1, sh2, sc2, g2 = jnp.split(mod, 6, axis=-1)

        h = rmsnorm(x, norm_g[i, 0]) * (1.0 + sc1) + sh1
        kind, j = i % N_MIXERS, i // N_MIXERS
        if kind == 0:
            m = short_conv_mixer(h, sc_w_in[j], sc_conv[j], sc_w_out[j])
        elif kind == 1:
            m = pooling_mixer(h, pool_w[j], pool_b[j], pool_scale[j])
        else:
            m = conformer_conv_module(h, cf_w_pw1[j], cf_b_pw1[j], cf_w_dw[j], cf_b_dw[j],
                                      cf_ln_g[j], cf_ln_b[j], cf_w_pw2[j], cf_b_pw2[j])
        x = x + g1 * rmsnorm(m, norm_g[i, 1])

        h = rmsnorm(x, norm_g[i, 2]) * (1.0 + sc2) + sh2
        f = conv_ffn(h, ffn_w_up[i], ffn_conv[i], ffn_b_conv[i], ffn_w_down[i])
        x = x + g2 * rmsnorm(f, norm_g[i, 3])
    return x
```

```python
import functools

import jax
import jax.numpy as jnp
from jax import lax
from jax.experimental import pallas as pl
from jax.experimental.pallas import tpu as pltpu

D_MODEL = 1024
SEQ = 16384
DEPTH = 4
N_MIXERS = 3
SHORT_CONV_WIDTH = 3
POOL_WINDOWS = (2, 4, 8, 16)
POOL_GROUP_DIM = D_MODEL // len(POOL_WINDOWS)
CONFORMER_CONV_WIDTH = 31
FFN_DIM = 2816
FFN_CONV_WIDTH = 3
RMS_EPS = 1e-6
LN_EPS = 1e-5

V7X_SUBLANES = 8
V7X_MXU_DIM = 256
V7X_VMEM_BYTES = 64 * 1024 * 1024

TM = 512
FFN_CHUNKS = ((0, 768), (768, 768), (1536, 768), (2304, 512))
HALO_SHORT = V7X_SUBLANES
HALO_POOL = 16
HALO_CONF = 32
VMEM_LIMIT = V7X_VMEM_BYTES - 6 * 1024 * 1024

_f32 = jnp.float32
_bf16 = jnp.bfloat16


def _dot(a, b):
    return jnp.dot(a, b, preferred_element_type=_f32)


def _rms(x, g):
    ms = jnp.mean(x * x, axis=-1, keepdims=True)
    return x * lax.rsqrt(ms + RMS_EPS) * g


def _silu(x):
    return x * jax.nn.sigmoid(x)


def _mod_kernel(c_ref, w_ref, b_ref, o_ref):
    ca = _silu(c_ref[...])
    lhs = jnp.broadcast_to(ca, (V7X_SUBLANES, D_MODEL)).astype(_bf16)
    r = _dot(lhs, w_ref[0].astype(_bf16))
    o_ref[0] = r[0:1, :] + b_ref[0]


def _modulation(c, w_mod, b_mod):
    n_vec = 6
    out = pl.pallas_call(
        _mod_kernel,
        grid=(DEPTH, n_vec),
        in_specs=[
            pl.BlockSpec((1, D_MODEL), lambda i, j: (0, 0)),
            pl.BlockSpec((1, D_MODEL, D_MODEL), lambda i, j: (i, 0, j)),
            pl.BlockSpec((1, 1, D_MODEL), lambda i, j: (i * n_vec + j, 0, 0)),
        ],
        out_specs=pl.BlockSpec((1, 1, D_MODEL), lambda i, j: (i * n_vec + j, 0, 0)),
        out_shape=jax.ShapeDtypeStruct((DEPTH * n_vec, 1, D_MODEL), _f32),
        compiler_params=pltpu.CompilerParams(
            dimension_semantics=("arbitrary", "arbitrary")),
        name="adaln_modulation",
    )(c, w_mod, b_mod.reshape(DEPTH * n_vec, 1, D_MODEL))
    return out.reshape(DEPTH, n_vec, D_MODEL)


def _carry_tail(buf, halo):
    buf[0:halo, :] = buf[TM:TM + halo, :]


def _ffn(x1, mod_ref, ng_ref, wup_ref, fcw_ref, fcb_ref, wdn_ref, gbuf):
    h2 = (_rms(x1, ng_ref[2:3, :]) * (1.0 + mod_ref[4:5, :]) + mod_ref[3:4, :]).astype(_bf16)
    acc = None
    for a, w in FFN_CHUNKS:
        gate = _dot(h2, wup_ref[:, a:a + w])
        val = _dot(h2, wup_ref[:, FFN_DIM + a:FFN_DIM + a + w])
        gbuf[HALO_SHORT:HALO_SHORT + TM, a:a + w] = gate
        gc = (fcw_ref[0:1, a:a + w] * gbuf[HALO_SHORT - 2:HALO_SHORT - 2 + TM, a:a + w]
              + fcw_ref[1:2, a:a + w] * gbuf[HALO_SHORT - 1:HALO_SHORT - 1 + TM, a:a + w]
              + fcw_ref[2:3, a:a + w] * gate
              + fcb_ref[:, a:a + w])
        p = (_silu(gc) * val).astype(_bf16)
        d = _dot(p, wdn_ref[a:a + w, :])
        acc = d if acc is None else acc + d
    _carry_tail(gbuf, HALO_SHORT)
    return x1 + mod_ref[5:6, :] * _rms(acc, ng_ref[3:4, :])


def _zero_halos_on_first_tile(*bufs_and_halos):
    @pl.when(pl.program_id(0) == 0)
    def _():
        for buf, halo in bufs_and_halos:
            buf[0:halo, :] = jnp.zeros((halo, buf.shape[1]), buf.dtype)


def _short_conv_layer(x_ref, mod_ref, ng_ref, win_ref, cw_ref, wout_ref,
                      wup_ref, fcw_ref, fcb_ref, wdn_ref, o_ref, vbuf, gbuf):
    _zero_halos_on_first_tile((vbuf, HALO_SHORT), (gbuf, HALO_SHORT))
    x = x_ref[...]
    h = (_rms(x, ng_ref[0:1, :]) * (1.0 + mod_ref[1:2, :]) + mod_ref[0:1, :]).astype(_bf16)
    proj = _dot(h, win_ref[...])
    v = proj[:, D_MODEL:2 * D_MODEL] * proj[:, 2 * D_MODEL:]
    vbuf[HALO_SHORT:HALO_SHORT + TM, :] = v
    u = (cw_ref[0:1, :] * vbuf[HALO_SHORT - 2:HALO_SHORT - 2 + TM, :]
         + cw_ref[1:2, :] * vbuf[HALO_SHORT - 1:HALO_SHORT - 1 + TM, :]
         + cw_ref[2:3, :] * v)
    y = (proj[:, :D_MODEL] * u).astype(_bf16)
    m = _dot(y, wout_ref[...])
    _carry_tail(vbuf, HALO_SHORT)
    x1 = x + mod_ref[2:3, :] * _rms(m, ng_ref[1:2, :])
    o_ref[...] = _ffn(x1, mod_ref, ng_ref, wup_ref, fcw_ref, fcb_ref, wdn_ref, gbuf)


def _pool_layer(x_ref, mod_ref, ng_ref, pw_ref, pb_ref, ps_ref,
                wup_ref, fcw_ref, fcb_ref, wdn_ref, o_ref, hbuf, gbuf):
    _zero_halos_on_first_tile((hbuf, HALO_POOL), (gbuf, HALO_SHORT))
    x = x_ref[...]
    h = _rms(x, ng_ref[0:1, :]) * (1.0 + mod_ref[1:2, :]) + mod_ref[0:1, :]
    hbuf[HALO_POOL:HALO_POOL + TM, :] = h
    t_idx = pl.program_id(0) * TM + lax.broadcasted_iota(jnp.int32, (TM, 1), 0)
    ys = []
    for g, window in enumerate(POOL_WINDOWS):
        c0 = g * POOL_GROUP_DIM
        cols = slice(c0, c0 + POOL_GROUP_DIM)
        ext = hbuf[:, cols]
        n = HALO_POOL + TM
        span = 1
        s = ext
        while span < window:
            s = s[span:, :] + s[:n - span, :]
            n -= span
            span *= 2
        start = HALO_POOL - window + 1
        tsum = s[start:start + TM, :]
        count = jnp.minimum(t_idx + 1, window).astype(_f32)
        pooled = tsum / count - h[:, cols]
        ys.append(_dot(pooled.astype(_bf16), pw_ref[g]))
    y = jnp.concatenate(ys, axis=-1)
    m = (y + pb_ref[...]) * ps_ref[...]
    _carry_tail(hbuf, HALO_POOL)
    x1 = x + mod_ref[2:3, :] * _rms(m, ng_ref[1:2, :])
    o_ref[...] = _ffn(x1, mod_ref, ng_ref, wup_ref, fcw_ref, fcb_ref, wdn_ref, gbuf)


def _conformer_layer(x_ref, mod_ref, ng_ref, pw1_ref, pb1_ref, dw_ref, db_ref,
                     lng_ref, lnb_ref, pw2_ref, pb2_ref,
                     wup_ref, fcw_ref, fcb_ref, wdn_ref, o_ref, ubuf, gbuf):
    _zero_halos_on_first_tile((ubuf, HALO_CONF), (gbuf, HALO_SHORT))
    x = x_ref[...]
    h = (_rms(x, ng_ref[0:1, :]) * (1.0 + mod_ref[1:2, :]) + mod_ref[0:1, :]).astype(_bf16)
    ag = _dot(h, pw1_ref[...]) + pb1_ref[...]
    u = ag[:, :D_MODEL] * jax.nn.sigmoid(ag[:, D_MODEL:])
    ubuf[HALO_CONF:HALO_CONF + TM, :] = u
    acc = db_ref[...] + dw_ref[CONFORMER_CONV_WIDTH - 1:CONFORMER_CONV_WIDTH, :] * u
    for k in range(CONFORMER_CONV_WIDTH - 1):
        r0 = HALO_CONF - (CONFORMER_CONV_WIDTH - 1) + k
        acc = acc + dw_ref[k:k + 1, :] * ubuf[r0:r0 + TM, :]
    mu = jnp.mean(acc, axis=-1, keepdims=True)
    cen = acc - mu
    var = jnp.mean(cen * cen, axis=-1, keepdims=True)
    ln = cen * lax.rsqrt(var + LN_EPS) * lng_ref[...] + lnb_ref[...]
    m = _dot(_silu(ln).astype(_bf16), pw2_ref[...]) + pb2_ref[...]
    _carry_tail(ubuf, HALO_CONF)
    x1 = x + mod_ref[2:3, :] * _rms(m, ng_ref[1:2, :])
    o_ref[...] = _ffn(x1, mod_ref, ng_ref, wup_ref, fcw_ref, fcb_ref, wdn_ref, gbuf)


def _resident(shape):
    nd = len(shape)
    return pl.BlockSpec(shape, lambda t: (0,) * nd, pipeline_mode=pl.Buffered(1))


def _run_layer(body, name, x, small, scratch):
    n_tiles = SEQ // TM
    tile = pl.BlockSpec((TM, D_MODEL), lambda t: (t, 0))
    return pl.pallas_call(
        body,
        grid=(n_tiles,),
        in_specs=[tile] + [_resident(a.shape) for a in small],
        out_specs=tile,
        out_shape=jax.ShapeDtypeStruct((SEQ, D_MODEL), _f32),
        scratch_shapes=scratch,
        compiler_params=pltpu.CompilerParams(
            dimension_semantics=("arbitrary",),
            vmem_limit_bytes=VMEM_LIMIT),
        name=name,
    )(x, *small)


def kernel(x, c, w_mod, b_mod, norm_g, sc_w_in, sc_conv, sc_w_out, pool_w, pool_b, pool_scale, cf_w_pw1, cf_b_pw1, cf_w_dw, cf_b_dw, cf_ln_g, cf_ln_b, cf_w_pw2, cf_b_pw2, ffn_w_up, ffn_conv, ffn_b_conv, ffn_w_down):
    assert x.shape == (1, SEQ, D_MODEL) and SEQ % TM == 0
    mod = _modulation(c, w_mod, b_mod)
    xs = x.reshape(SEQ, D_MODEL)
    gbuf = pltpu.VMEM((HALO_SHORT + TM, FFN_DIM), _f32)
    row = lambda v: v.reshape(1, -1)
    for i in range(DEPTH):
        kind, j = i % N_MIXERS, i // N_MIXERS
        ffn = [ffn_w_up[i].astype(_bf16), ffn_conv[i], row(ffn_b_conv[i]),
               ffn_w_down[i].astype(_bf16)]
        head = [mod[i], norm_g[i]]
        if kind == 0:
            small = head + [sc_w_in[j].astype(_bf16), sc_conv[j], sc_w_out[j].astype(_bf16)] + ffn
            scratch = [pltpu.VMEM((HALO_SHORT + TM, D_MODEL), _f32), gbuf]
            xs = _run_layer(_short_conv_layer, f"short_conv_layer_{i}", xs, small, scratch)
        elif kind == 1:
            small = head + [pool_w[j].astype(_bf16), row(pool_b[j]), row(pool_scale[j])] + ffn
            scratch = [pltpu.VMEM((HALO_POOL + TM, D_MODEL), _f32), gbuf]
            xs = _run_layer(_pool_layer, f"pool_layer_{i}", xs, small, scratch)
        else:
            small = head + [cf_w_pw1[j].astype(_bf16), row(cf_b_pw1[j]), cf_w_dw[j], row(cf_b_dw[j]),
                            row(cf_ln_g[j]), row(cf_ln_b[j]), cf_w_pw2[j].astype(_bf16),
                            row(cf_b_pw2[j])] + ffn
            scratch = [pltpu.VMEM((HALO_CONF + TM, D_MODEL), _f32), gbuf]
            xs = _run_layer(_conformer_layer, f"conformer_layer_{i}", xs, small, scratch)
    return xs.reshape(1, SEQ, D_MODEL)
```

```python
import jax
import jax.numpy as jnp
from jax import lax
from jax.experimental import pallas as pl
from jax.experimental.pallas import tpu as pltpu

D_MODEL = 1024
SEQ = 16384
DEPTH = 4
N_MIXERS = 3
N_MOD = 6
SHORT_CONV_WIDTH = 3
POOL_WINDOWS = (2, 4, 8, 16)
POOL_GROUP_DIM = D_MODEL // len(POOL_WINDOWS)
CONFORMER_CONV_WIDTH = 31
FFN_DIM = 2816
FFN_CONV_WIDTH = 3
RMS_EPS = 1e-6
LN_EPS = 1e-5

V7X_SUBLANES = 8
V7X_VMEM_BYTES = 64 * 1024 * 1024

TM = 512
N_TILES = SEQ // TM
FFN_CHUNKS = ((0, 768), (768, 768), (1536, 768), (2304, 512))
HALO_SHORT = V7X_SUBLANES
HALO_POOL = 16
HALO_CONF = 32
VMEM_LIMIT = V7X_VMEM_BYTES - 6 * 1024 * 1024

_f32 = jnp.float32
_bf16 = jnp.bfloat16


def _dot(a, b):
    return jnp.dot(a, b, preferred_element_type=_f32)


def _rms(x, g):
    ms = jnp.mean(x * x, axis=-1, keepdims=True)
    return x * lax.rsqrt(ms + RMS_EPS) * g


def _silu(x):
    return x * jax.nn.sigmoid(x)


def _back(ext, halo, r):
    if r % V7X_SUBLANES == 0:
        return ext[halo - r:halo - r + TM]
    return pltpu.roll(ext, r, 0)[halo:halo + TM]


def _mod_kernel(c_ref, w_ref, b_ref, o_ref):
    ca = _silu(c_ref[...])
    lhs = jnp.broadcast_to(ca, (V7X_SUBLANES, D_MODEL)).astype(_bf16)
    r = _dot(lhs, w_ref[0].astype(_bf16))
    o_ref[0] = r[0:1, :] + b_ref[0]


def _modulation(c, w_mod, b_mod):
    out = pl.pallas_call(
        _mod_kernel,
        grid=(DEPTH, N_MOD),
        in_specs=[
            pl.BlockSpec((1, D_MODEL), lambda i, j: (0, 0)),
            pl.BlockSpec((1, D_MODEL, D_MODEL), lambda i, j: (i, 0, j)),
            pl.BlockSpec((1, 1, D_MODEL), lambda i, j: (i * N_MOD + j, 0, 0)),
        ],
        out_specs=pl.BlockSpec((1, 1, D_MODEL), lambda i, j: (i * N_MOD + j, 0, 0)),
        out_shape=jax.ShapeDtypeStruct((DEPTH * N_MOD, 1, D_MODEL), _f32),
        compiler_params=pltpu.CompilerParams(
            dimension_semantics=("arbitrary", "arbitrary")),
        name="adaln_modulation",
    )(c, w_mod, b_mod.reshape(DEPTH * N_MOD, 1, D_MODEL))
    return out.reshape(DEPTH, N_MOD, D_MODEL)


def _ffn(x1, mod_ref, ng_ref, wup_ref, fcw_ref, fcb_ref, wdn_ref, ghalo):
    h2 = (_rms(x1, ng_ref[2:3, :]) * (1.0 + mod_ref[4:5, :]) + mod_ref[3:4, :]).astype(_bf16)
    acc = None
    for a, w in FFN_CHUNKS:
        gate = _dot(h2, wup_ref[:, a:a + w])
        val = _dot(h2, wup_ref[:, FFN_DIM + a:FFN_DIM + a + w])
        ext = jnp.concatenate([ghalo[:, a:a + w], gate], axis=0)
        ghalo[:, a:a + w] = gate[TM - HALO_SHORT:, :]
        gc = (fcw_ref[0:1, a:a + w] * _back(ext, HALO_SHORT, 2)
              + fcw_ref[1:2, a:a + w] * _back(ext, HALO_SHORT, 1)
              + fcw_ref[2:3, a:a + w] * gate
              + fcb_ref[:, a:a + w])
        p = (_silu(gc) * val).astype(_bf16)
        d = _dot(p, wdn_ref[a:a + w, :])
        acc = d if acc is None else acc + d
    return x1 + mod_ref[5:6, :] * _rms(acc, ng_ref[3:4, :])


def _ffn_of_previous_tile(x1buf, ghalo, mix_halo, ffn_refs, mod_ref, ng_ref, o_ref):
    t = pl.program_id(0)

    @pl.when(t == 0)
    def _():
        x1buf[...] = jnp.zeros(x1buf.shape, x1buf.dtype)
        mix_halo[...] = jnp.zeros(mix_halo.shape, mix_halo.dtype)

    @pl.when(t <= 1)
    def _():
        ghalo[...] = jnp.zeros(ghalo.shape, ghalo.dtype)

    o_ref[...] = _ffn(x1buf[...], mod_ref, ng_ref, *ffn_refs, ghalo)


def _adaln_in(x, mod_ref, ng_ref):
    return _rms(x, ng_ref[0:1, :]) * (1.0 + mod_ref[1:2, :]) + mod_ref[0:1, :]


def _mixer_out(x, m, mod_ref, ng_ref):
    return x + mod_ref[2:3, :] * _rms(m, ng_ref[1:2, :])


def _short_conv_layer(x_ref, mod_ref, ng_ref, win_ref, cw_ref, wout_ref,
                      wup_ref, fcw_ref, fcb_ref, wdn_ref, o_ref, x1buf, ghalo, vhalo):
    _ffn_of_previous_tile(x1buf, ghalo, vhalo, (wup_ref, fcw_ref, fcb_ref, wdn_ref),
                          mod_ref, ng_ref, o_ref)
    x = x_ref[...]
    h = _adaln_in(x, mod_ref, ng_ref).astype(_bf16)
    proj = _dot(h, win_ref[...])
    v = proj[:, D_MODEL:2 * D_MODEL] * proj[:, 2 * D_MODEL:]
    ext = jnp.concatenate([vhalo[...], v], axis=0)
    vhalo[...] = v[TM - HALO_SHORT:, :]
    u = (cw_ref[0:1, :] * _back(ext, HALO_SHORT, 2)
         + cw_ref[1:2, :] * _back(ext, HALO_SHORT, 1)
         + cw_ref[2:3, :] * v)
    y = (proj[:, :D_MODEL] * u).astype(_bf16)
    m = _dot(y, wout_ref[...])
    x1buf[...] = _mixer_out(x, m, mod_ref, ng_ref)


def _pool_layer(x_ref, mod_ref, ng_ref, pw_ref, pb_ref, ps_ref,
                wup_ref, fcw_ref, fcb_ref, wdn_ref, o_ref, x1buf, ghalo, hhalo):
    _ffn_of_previous_tile(x1buf, ghalo, hhalo, (wup_ref, fcw_ref, fcb_ref, wdn_ref),
                          mod_ref, ng_ref, o_ref)
    x = x_ref[...]
    h = _adaln_in(x, mod_ref, ng_ref)
    ext = jnp.concatenate([hhalo[...], h], axis=0)
    hhalo[...] = h[TM - HALO_POOL:, :]
    t_idx = (jnp.minimum(pl.program_id(0), N_TILES - 1) * TM
             + lax.broadcasted_iota(jnp.int32, (TM, 1), 0))
    ys = []
    for g, window in enumerate(POOL_WINDOWS):
        cols = slice(g * POOL_GROUP_DIM, (g + 1) * POOL_GROUP_DIM)
        s = ext[:, cols]
        span = 1
        while span < window:
            if span % V7X_SUBLANES == 0:
                s = jnp.concatenate([s[:span], s[span:] + s[:-span]], axis=0)
            else:
                s = s + pltpu.roll(s, span, 0)
            span *= 2
        inv_count = 1.0 / jnp.minimum(t_idx + 1, window).astype(_f32)
        pooled = s[HALO_POOL:] * inv_count - h[:, cols]
        ys.append(_dot(pooled.astype(_bf16), pw_ref[g]))
    y = jnp.concatenate(ys, axis=-1)
    m = (y + pb_ref[...]) * ps_ref[...]
    x1buf[...] = _mixer_out(x, m, mod_ref, ng_ref)


def _conformer_layer(x_ref, mod_ref, ng_ref, pw1_ref, pb1_ref, dw_ref, db_ref,
                     lng_ref, lnb_ref, pw2_ref, pb2_ref,
                     wup_ref, fcw_ref, fcb_ref, wdn_ref, o_ref, x1buf, ghalo, uhalo):
    _ffn_of_previous_tile(x1buf, ghalo, uhalo, (wup_ref, fcw_ref, fcb_ref, wdn_ref),
                          mod_ref, ng_ref, o_ref)
    x = x_ref[...]
    h = _adaln_in(x, mod_ref, ng_ref).astype(_bf16)
    ag = _dot(h, pw1_ref[...]) + pb1_ref[...]
    u = ag[:, :D_MODEL] * jax.nn.sigmoid(ag[:, D_MODEL:])
    ext = jnp.concatenate([uhalo[...], u], axis=0)
    uhalo[...] = u[TM - HALO_CONF:, :]
    rolled = [ext] + [pltpu.roll(ext, r, 0) for r in range(1, V7X_SUBLANES)]
    acc = db_ref[...] + dw_ref[CONFORMER_CONV_WIDTH - 1:CONFORMER_CONV_WIDTH, :] * u
    for k in range(CONFORMER_CONV_WIDTH - 1):
        q, r = divmod(CONFORMER_CONV_WIDTH - 1 - k, V7X_SUBLANES)
        lo = HALO_CONF - V7X_SUBLANES * q
        acc = acc + dw_ref[k:k + 1, :] * rolled[r][lo:lo + TM]
    mu = jnp.mean(acc, axis=-1, keepdims=True)
    cen = acc - mu
    var = jnp.mean(cen * cen, axis=-1, keepdims=True)
    ln = cen * lax.rsqrt(var + LN_EPS) * lng_ref[...] + lnb_ref[...]
    m = _dot(_silu(ln).astype(_bf16), pw2_ref[...]) + pb2_ref[...]
    x1buf[...] = _mixer_out(x, m, mod_ref, ng_ref)


def _whole(a):
    nd = a.ndim
    return pl.BlockSpec(a.shape, lambda t: (0,) * nd, pipeline_mode=pl.Buffered(1))


def _layer_of(a, i):
    nd = a.ndim
    return pl.BlockSpec((None,) + a.shape[1:], lambda t: (i,) + (0,) * (nd - 1),
                        pipeline_mode=pl.Buffered(1))


def _run_layer(body, name, x, operands, specs, halo_rows):
    x_in = pl.BlockSpec((TM, D_MODEL), lambda t: (jnp.minimum(t, N_TILES - 1), 0))
    x_out = pl.BlockSpec((TM, D_MODEL), lambda t: (jnp.maximum(t - 1, 0), 0))
    scratch = [pltpu.VMEM((TM, D_MODEL), _f32),
               pltpu.VMEM((HALO_SHORT, FFN_DIM), _f32),
               pltpu.VMEM((halo_rows, D_MODEL), _f32)]
    return pl.pallas_call(
        body,
        grid=(N_TILES + 1,),
        in_specs=[x_in] + specs,
        out_specs=x_out,
        out_shape=jax.ShapeDtypeStruct((SEQ, D_MODEL), _f32),
        scratch_shapes=scratch,
        compiler_params=pltpu.CompilerParams(
            dimension_semantics=("arbitrary",),
            vmem_limit_bytes=VMEM_LIMIT),
        name=name,
    )(x, *operands)


def kernel(x, c, w_mod, b_mod, norm_g, sc_w_in, sc_conv, sc_w_out, pool_w, pool_b, pool_scale, cf_w_pw1, cf_b_pw1, cf_w_dw, cf_b_dw, cf_ln_g, cf_ln_b, cf_w_pw2, cf_b_pw2, ffn_w_up, ffn_conv, ffn_b_conv, ffn_w_down):
    assert x.shape == (1, SEQ, D_MODEL) and SEQ % TM == 0
    mod = _modulation(c, w_mod, b_mod)
    xs = x.reshape(SEQ, D_MODEL)
    row3 = lambda v: v.reshape(v.shape[0], 1, v.shape[1])
    ffn_all = [ffn_w_up.astype(_bf16), ffn_conv, row3(ffn_b_conv), ffn_w_down.astype(_bf16)]
    sc_all = [sc_w_in.astype(_bf16), sc_conv, sc_w_out.astype(_bf16)]
    pool_all = [pool_w.astype(_bf16), row3(pool_b), row3(pool_scale)]
    cf_all = [cf_w_pw1.astype(_bf16), row3(cf_b_pw1), cf_w_dw, row3(cf_b_dw), row3(cf_ln_g),
              row3(cf_ln_b), cf_w_pw2.astype(_bf16), row3(cf_b_pw2)]
    for i in range(DEPTH):
        kind, j = i % N_MIXERS, i // N_MIXERS
        head = [mod, norm_g]
        head_specs = [_layer_of(mod, i), _layer_of(norm_g, i)]
        ffn_specs = [_layer_of(a, i) for a in ffn_all]
        if kind == 0:
            body, name, mix, halo = _short_conv_layer, f"short_conv_layer_{i}", sc_all, HALO_SHORT
        elif kind == 1:
            body, name, mix, halo = _pool_layer, f"pool_layer_{i}", pool_all, HALO_POOL
        else:
            body, name, mix, halo = _conformer_layer, f"conformer_layer_{i}", cf_all, HALO_CONF
        specs = head_specs + [_layer_of(a, j) for a in mix] + ffn_specs
        xs = _run_layer(body, name, xs, head + mix + ffn_all, specs, halo)
    return xs.reshape(1, SEQ, D_MODEL)
```
